```python
import math
import jax, jax.numpy as jnp
from jax import lax
import numpy as np

D_MODEL = 4096
BATCH = 4
SEQ = 2048
DEPTH = 2
DEC_BATCH = 8
DEC_SEQ = 1
PAST_LEN = 16384
PAGE_SIZE = 128

N_MIXERS = 2
N_ATTN_LAYERS = (DEPTH + 1) // 2
N_RET_LAYERS = DEPTH // 2

ATTN_HEADS = 32
ATTN_HEAD_DIM = D_MODEL // ATTN_HEADS
ATTN_WIDTH = ATTN_HEADS * ATTN_HEAD_DIM
MOBA_BLOCK = 256
MOBA_TOPK = 3
Q_CHUNK = 8

RET_HEADS = 16
RET_KEY_DIM = D_MODEL // RET_HEADS
RET_VALUE_DIM = 2 * D_MODEL // RET_HEADS
RET_QK_WIDTH = RET_HEADS * RET_KEY_DIM
RET_V_WIDTH = RET_HEADS * RET_VALUE_DIM
RET_CHUNK = 128

NORM_EPS = 1e-6

kernel_name = 'moba_retention_hybrid_step'


def _rmsnorm(x, g):
    xf = x.astype(jnp.float32)
    y = xf * lax.rsqrt(jnp.mean(xf * xf, axis=-1, keepdims=True) + NORM_EPS)
    return (y * g.astype(jnp.float32)).astype(x.dtype)


def _head_rmsnorm(o, g):
    y = o * lax.rsqrt(jnp.mean(o * o, axis=-1, keepdims=True) + NORM_EPS)
    return y.reshape(o.shape[:-2] + (-1,)) * g.astype(jnp.float32)


def _alibi_slopes():
    return jnp.exp2(-8.0 * jnp.arange(1, ATTN_HEADS + 1, dtype=jnp.float32) / ATTN_HEADS)


def _retention_log_decay():
    return jnp.log1p(-jnp.exp2(-5.0 - jnp.arange(RET_HEADS, dtype=jnp.float32)))


def _gated_out(o, g, w_out):
    return (o.astype(g.dtype) * jax.nn.silu(g)) @ w_out


def _moba_project(h, w_in):
    proj = h @ w_in
    q, k, v, g = jnp.split(proj, [ATTN_WIDTH, 2 * ATTN_WIDTH, 3 * ATTN_WIDTH], axis=-1)
    shp = h.shape[:-1] + (ATTN_HEADS, ATTN_HEAD_DIM)
    return q.reshape(shp), k.reshape(shp), v.reshape(shp), g


def _moba_attend(q, tq, k_sel, v_sel, sel, sel_valid, k_own, v_own, pos_own, slopes):
    b, nq, h, dh = q.shape
    scale = dh ** -0.5
    d_own = tq[:, None] - pos_own[None, :]
    s_own = jnp.einsum('bqhd,blhd->bqhl', q, k_own).astype(jnp.float32) * scale
    s_own = jnp.where((d_own >= 0)[None, :, None, :],
                      s_own - slopes[None, None, :, None] * d_own.astype(jnp.float32)[None, :, None, :],
                      -jnp.inf)
    if k_sel is None:
        p = jax.nn.softmax(s_own, axis=-1).astype(v_own.dtype)
        return jnp.einsum('bqhl,blhd->bqhd', p, v_own)
    s_sel = jnp.einsum('bqhd,bqhnsd->bqhns', q, k_sel).astype(jnp.float32) * scale
    pos_sel = sel[..., None] * MOBA_BLOCK + jnp.arange(MOBA_BLOCK, dtype=jnp.int32)
    d_sel = (tq[None, :, None, None, None] - pos_sel).astype(jnp.float32)
    s_sel = jnp.where(sel_valid[..., None], s_sel - slopes[None, None, :, None, None] * d_sel, -jnp.inf)
    n_sel = s_sel.shape[3] * s_sel.shape[4]
    p = jax.nn.softmax(jnp.concatenate([s_sel.reshape(b, nq, h, n_sel), s_own], axis=-1), axis=-1)
    p = p.astype(v_own.dtype)
    out_sel = jnp.einsum('bqhns,bqhnsd->bqhd', p[..., :n_sel].reshape(s_sel.shape), v_sel)
    out_own = jnp.einsum('bqhl,blhd->bqhd', p[..., n_sel:], v_own)
    return out_sel + out_own


def _moba_prompt(q, k, v, slopes):
    b, s, h, dh = q.shape
    nb_full = s // MOBA_BLOCK
    nb_pad = -(-s // MOBA_BLOCK)
    pad = nb_pad * MOBA_BLOCK - s
    kb = jnp.pad(k, ((0, 0), (0, pad), (0, 0), (0, 0))).reshape(b, nb_pad, MOBA_BLOCK, h, dh)
    vb = jnp.pad(v, ((0, 0), (0, pad), (0, 0), (0, 0))).reshape(b, nb_pad, MOBA_BLOCK, h, dh)
    topk = min(MOBA_TOPK, nb_full)
    n_chunks = s // Q_CHUNK

    def chunks(a):
        return jnp.moveaxis(a.reshape((b, n_chunks, Q_CHUNK) + a.shape[2:]), 1, 0)

    starts = jnp.arange(n_chunks, dtype=jnp.int32) * Q_CHUNK
    if topk > 0:
        k_mean = jnp.mean(kb[:, :nb_full].astype(jnp.float32), axis=2)
        gate = jnp.einsum('bshd,bnhd->bshn', q.astype(jnp.float32), k_mean)
        q_blk = jnp.arange(s, dtype=jnp.int32) // MOBA_BLOCK
        past = jnp.arange(nb_full, dtype=jnp.int32)[None, :] < q_blk[:, None]
        gate = jnp.where(past[None, :, None, :], gate, -jnp.inf)
        _, sel = lax.top_k(gate, topk)
        sel_valid = jnp.arange(topk, dtype=jnp.int32)[None, None, None, :] < q_blk[None, :, None, None]
        sel_valid = jnp.broadcast_to(sel_valid, sel.shape)
        xs = (chunks(q), starts, chunks(sel), chunks(sel_valid))
    else:
        xs = (chunks(q), starts, None, None)
    b_idx = jnp.arange(b)[:, None, None, None]
    h_idx = jnp.arange(h)[None, None, :, None]

    def one_chunk(args):
        qc, start, selc, validc = args
        tq = start + jnp.arange(Q_CHUNK, dtype=jnp.int32)
        own = start // MOBA_BLOCK
        k_own = lax.dynamic_index_in_dim(kb, own, axis=1, keepdims=False)
        v_own = lax.dynamic_index_in_dim(vb, own, axis=1, keepdims=False)
        pos_own = own * MOBA_BLOCK + jnp.arange(MOBA_BLOCK, dtype=jnp.int32)
        if selc is None:
            return _moba_attend(qc, tq, None, None, None, None, k_own, v_own, pos_own, slopes)
        k_sel = kb[b_idx, selc, :, h_idx, :]
        v_sel = vb[b_idx, selc, :, h_idx, :]
        return _moba_attend(qc, tq, k_sel, v_sel, selc, validc, k_own, v_own, pos_own, slopes)

    out = lax.map(one_chunk, xs)
    return jnp.moveaxis(out, 0, 1).reshape(b, s, h, dh)


def _moba_sample(q, k_new, v_new, cache_k, cache_v, layer, page_table, slopes):
    b, t, h, dh = q.shape
    ppb = MOBA_BLOCK // PAGE_SIZE
    nb_full = PAST_LEN // MOBA_BLOCK
    own_first_page = nb_full * ppb
    own_past_pages = PAST_LEN // PAGE_SIZE - own_first_page
    topk = min(MOBA_TOPK, nb_full)
    tq = PAST_LEN + jnp.arange(t, dtype=jnp.int32)
    own_pages = page_table[:, own_first_page:own_first_page + own_past_pages]
    k_own = jnp.concatenate([cache_k[layer, own_pages].reshape(b, own_past_pages * PAGE_SIZE, h, dh).astype(k_new.dtype), k_new], axis=1)
    v_own = jnp.concatenate([cache_v[layer, own_pages].reshape(b, own_past_pages * PAGE_SIZE, h, dh).astype(v_new.dtype), v_new], axis=1)
    pos_own = nb_full * MOBA_BLOCK + jnp.arange(k_own.shape[1], dtype=jnp.int32)
    if topk == 0:
        return _moba_attend(q, tq, None, None, None, None, k_own, v_own, pos_own, slopes)
    page_sum = jnp.sum(cache_k[layer].astype(jnp.float32), axis=1)
    blk_sum = page_sum[page_table[:, :own_first_page]].reshape(b, nb_full, ppb, h, dh).sum(axis=2)
    k_mean = blk_sum / MOBA_BLOCK
    gate = jnp.einsum('bthd,bnhd->bthn', q.astype(jnp.float32), k_mean)
    _, sel = lax.top_k(gate, topk)
    b_idx = jnp.arange(b)[:, None, None, None, None]
    h_idx = jnp.arange(h)[None, None, :, None, None]
    phys = page_table[b_idx, sel[..., None] * ppb + jnp.arange(ppb, dtype=jnp.int32)]
    k_sel = cache_k[layer, phys, :, h_idx, :].reshape(b, t, h, topk, MOBA_BLOCK, dh).astype(k_new.dtype)
    v_sel = cache_v[layer, phys, :, h_idx, :].reshape(b, t, h, topk, MOBA_BLOCK, dh).astype(v_new.dtype)
    sel_valid = jnp.ones(sel.shape, dtype=bool)
    return _moba_attend(q, tq, k_sel, v_sel, sel, sel_valid, k_own, v_own, pos_own, slopes)


def _retention_project(h, w_in):
    proj = h @ w_in
    q, k, v, g = jnp.split(proj, [RET_QK_WIDTH, 2 * RET_QK_WIDTH, 2 * RET_QK_WIDTH + RET_V_WIDTH], axis=-1)
    lead = h.shape[:-1]
    q = q.reshape(lead + (RET_HEADS, RET_KEY_DIM))
    k = k.reshape(lead + (RET_HEADS, RET_KEY_DIM)) * (RET_KEY_DIM ** -0.5)
    v = v.reshape(lead + (RET_HEADS, RET_VALUE_DIM))
    return q, k, v, g


def _retention_chunk(state, q, k, v, log_gamma):
    c = q.shape[1]
    idx = jnp.arange(c, dtype=jnp.float32)
    diff = idx[:, None] - idx[None, :]
    causal = diff >= 0
    decay = jnp.where(causal[None], jnp.exp(jnp.where(causal, diff, 0.0)[None] * log_gamma[:, None, None]), 0.0)
    qf, kf, vf = q.astype(jnp.float32), k.astype(jnp.float32), v.astype(jnp.float32)
    scores = jnp.einsum('bihd,bjhd->bhij', qf, kf) * decay[None]
    o_inner = jnp.einsum('bhij,bjhe->bihe', scores, vf)
    q_decay = jnp.exp((idx + 1.0)[:, None] * log_gamma[None, :])
    o_cross = jnp.einsum('bihd,bhde->bihe', qf, state) * q_decay[None, :, :, None]
    k_decay = jnp.exp((c - 1.0 - idx)[:, None] * log_gamma[None, :])
    new_state = jnp.exp(c * log_gamma)[None, :, None, None] * state + \
        jnp.einsum('bjhd,bjhe->bhde', kf * k_decay[None, :, :, None], vf)
    return new_state, o_inner + o_cross


def _retention_prompt(h, w_in, w_out, g_norm, log_gamma):
    b, s, _ = h.shape
    q, k, v, g = _retention_project(h, w_in)
    n_chunks = s // RET_CHUNK

    def chunks(a):
        return jnp.moveaxis(a.reshape((b, n_chunks, RET_CHUNK) + a.shape[2:]), 1, 0)

    state0 = jnp.zeros((b, RET_HEADS, RET_KEY_DIM, RET_VALUE_DIM), jnp.float32)

    def step(state, xs):
        qc, kc, vc = xs
        return _retention_chunk(state, qc, kc, vc, log_gamma)

    state, o = lax.scan(step, state0, (chunks(q), chunks(k), chunks(v)))
    o = jnp.moveaxis(o, 0, 1).reshape(b, s, RET_HEADS, RET_VALUE_DIM)
    y = _gated_out(_head_rmsnorm(o, g_norm), g, w_out)
    return y, state


def _retention_sample(h, state, w_in, w_out, g_norm, log_gamma):
    q, k, v, g = _retention_project(h, w_in)
    new_state, o = _retention_chunk(state.astype(jnp.float32), q, k, v, log_gamma)
    y = _gated_out(_head_rmsnorm(o, g_norm), g, w_out)
    return y, new_state


def setup_inputs(seed: int = 0) -> dict:
    key = jax.random.key(seed)
    ks = jax.random.split(key, 13)
    n_pages = PAST_LEN // PAGE_SIZE
    n_pool = (DEC_BATCH * n_pages * 5) // 4
    nrm = jax.random.normal
    x_prompt = nrm(ks[0], (BATCH, SEQ, D_MODEL), jnp.float32)
    x_sample = nrm(ks[1], (DEC_BATCH, DEC_SEQ, D_MODEL), jnp.float32)
    cache_k = nrm(ks[2], (N_ATTN_LAYERS, n_pool, PAGE_SIZE, ATTN_HEADS, ATTN_HEAD_DIM), jnp.float32)
    cache_v = nrm(ks[3], (N_ATTN_LAYERS, n_pool, PAGE_SIZE, ATTN_HEADS, ATTN_HEAD_DIM), jnp.float32)
    state_ret = 0.5 * nrm(ks[4], (N_RET_LAYERS, DEC_BATCH, RET_HEADS, RET_KEY_DIM, RET_VALUE_DIM), jnp.float32)
    perm = jax.random.permutation(ks[5], n_pool)
    page_table = perm[:DEC_BATCH * n_pages].reshape(DEC_BATCH, n_pages).astype(jnp.int32)
    norm_pre = 1.0 + 0.02 * nrm(ks[6], (DEPTH, D_MODEL), jnp.float32)
    norm_post = 1.0 + 0.02 * nrm(ks[7], (DEPTH, D_MODEL), jnp.float32)
    w_in_attn = nrm(ks[8], (N_ATTN_LAYERS, D_MODEL, 4 * ATTN_WIDTH), jnp.float32) * (D_MODEL ** -0.5)
    w_out_attn = nrm(ks[9], (N_ATTN_LAYERS, ATTN_WIDTH, D_MODEL), jnp.float32) * (ATTN_WIDTH ** -0.5)
    w_in_ret = nrm(ks[10], (N_RET_LAYERS, D_MODEL, 2 * RET_QK_WIDTH + 2 * RET_V_WIDTH), jnp.float32) * (D_MODEL ** -0.5)
    w_out_ret = nrm(ks[11], (N_RET_LAYERS, RET_V_WIDTH, D_MODEL), jnp.float32) * (RET_V_WIDTH ** -0.5)
    ret_norm = 1.0 + 0.02 * nrm(ks[12], (N_RET_LAYERS, RET_V_WIDTH), jnp.float32)
    return {'x_prompt': x_prompt, 'x_sample': x_sample, 'cache_k': cache_k, 'cache_v': cache_v,
            'state_ret': state_ret, 'page_table': page_table, 'norm_pre': norm_pre, 'norm_post': norm_post,
            'w_in_attn': w_in_attn, 'w_out_attn': w_out_attn, 'w_in_ret': w_in_ret, 'w_out_ret': w_out_ret,
            'ret_norm': ret_norm}


def reference(x_prompt, x_sample, cache_k, cache_v, state_ret, page_table, norm_pre, norm_post,
              w_in_attn, w_out_attn, w_in_ret, w_out_ret, ret_norm):
    slopes = _alibi_slopes()
    log_gamma = _retention_log_decay()
    bp, sp, _ = x_prompt.shape
    bs, ts, _ = x_sample.shape
    k_prompt_rows, v_prompt_rows, st_prompt = [], [], []
    k_sample_rows, v_sample_rows, st_sample = [], [], []
    for i in range(DEPTH):
        j = i // N_MIXERS
        hp = _rmsnorm(x_prompt, norm_pre[i])
        hs = _rmsnorm(x_sample, norm_pre[i])
        if i % N_MIXERS == 0:
            qp, kp, vp, gp = _moba_project(hp, w_in_attn[j])
            op = _moba_prompt(qp, kp, vp, slopes)
            yp = _gated_out(op.reshape(bp, sp, ATTN_WIDTH), gp, w_out_attn[j])
            qs, kn, vn, gs = _moba_project(hs, w_in_attn[j])
            os_ = _moba_sample(qs, kn, vn, cache_k, cache_v, j, page_table, slopes)
            ys = _gated_out(os_.reshape(bs, ts, ATTN_WIDTH), gs, w_out_attn[j])
            k_prompt_rows.append(kp.astype(cache_k.dtype))
            v_prompt_rows.append(vp.astype(cache_v.dtype))
            k_sample_rows.append(kn.astype(cache_k.dtype))
            v_sample_rows.append(vn.astype(cache_v.dtype))
        else:
            yp, stp = _retention_prompt(hp, w_in_ret[j], w_out_ret[j], ret_norm[j], log_gamma)
            ys, sts = _retention_sample(hs, state_ret[j], w_in_ret[j], w_out_ret[j], ret_norm[j], log_gamma)
            st_prompt.append(stp.astype(state_ret.dtype))
            st_sample.append(sts.astype(state_ret.dtype))
        x_prompt = x_prompt + _rmsnorm(yp, norm_post[i])
        x_sample = x_sample + _rmsnorm(ys, norm_post[i])
    new_k_prompt = jnp.stack(k_prompt_rows, axis=0)
    new_v_prompt = jnp.stack(v_prompt_rows, axis=0)
    new_state_prompt = jnp.stack(st_prompt, axis=0)
    new_k_sample = jnp.stack(k_sample_rows, axis=0)
    new_v_sample = jnp.stack(v_sample_rows, axis=0)
    new_state_sample = jnp.stack(st_sample, axis=0)
    return (x_prompt, x_sample, new_k_prompt, new_v_prompt, new_state_prompt, new_k_sample, new_v_sample, new_state_sample)
```

```python
import functools

import jax
import jax.numpy as jnp
from jax import lax
from jax.experimental import pallas as pl
from jax.experimental.pallas import tpu as pltpu

NORM_EPS = 1e-6
MOBA_BLOCK = 256
MOBA_BLOCK_SHIFT = MOBA_BLOCK.bit_length() - 1
MOBA_TOPK = 3
PAGE_SIZE = 128
PAGES_PER_BLOCK = MOBA_BLOCK // PAGE_SIZE
ATTN_HEAD_DIM = 128
RET_KEY_DIM = 256
RET_VALUE_DIM = 512
RET_CHUNK = 128

V7X_LANES = 128
V7X_SUBLANES = 8
V7X_VMEM_LIMIT_BYTES = 56 * 1024 * 1024
SAMPLE_HEAD_GROUP = 8

_BF16 = jnp.bfloat16
_F32 = jnp.float32
_NT = (((1,), (1,)), ((), ()))
_TN = (((0,), (0,)), ((), ()))


def _params(*semantics):
    return pltpu.CompilerParams(dimension_semantics=semantics, vmem_limit_bytes=V7X_VMEM_LIMIT_BYTES)


def _silu(g):
    return g * jax.nn.sigmoid(g)


def _rms(x):
    return x * lax.rsqrt(jnp.mean(x * x, axis=-1, keepdims=True) + NORM_EPS)


def _take_row(x, b):
    rid = lax.broadcasted_iota(jnp.int32, x.shape, 0)
    return jnp.sum(jnp.where(rid == b, x, 0.0), axis=0, keepdims=True)


def _put_row(ref, cols, b, row):
    old = ref[:, cols]
    rid = lax.broadcasted_iota(jnp.int32, old.shape, 0)
    ref[:, cols] = jnp.where(rid == b, jnp.broadcast_to(row, old.shape), old)


def _rmsnorm_kernel(x_ref, g_ref, o_ref):
    o_ref[...] = (_rms(x_ref[...]) * g_ref[...]).astype(o_ref.dtype)


def _rmsnorm(x, gain, out_dtype):
    m, d = x.shape
    bm = min(m, 256)
    return pl.pallas_call(
        _rmsnorm_kernel,
        grid=(m // bm,),
        in_specs=[pl.BlockSpec((bm, d), lambda i: (i, 0)), pl.BlockSpec((1, d), lambda i: (0, 0))],
        out_specs=pl.BlockSpec((bm, d), lambda i: (i, 0)),
        out_shape=jax.ShapeDtypeStruct((m, d), out_dtype),
        compiler_params=_params("parallel"),
    )(x, gain.reshape(1, d))


def _matmul_kernel(x_ref, w_ref, o_ref):
    o_ref[...] = jnp.dot(x_ref[...].astype(_BF16), w_ref[...], preferred_element_type=_F32)


def _matmul(x, w, col0, n):
    m, k = x.shape
    bm = min(m, 1024 if k <= 4096 else 512)
    bn = min(n, 1024 if k <= 4096 else 512)
    off = col0 // bn
    return pl.pallas_call(
        _matmul_kernel,
        grid=(m // bm, n // bn),
        in_specs=[pl.BlockSpec((bm, k), lambda i, j: (i, 0)),
                  pl.BlockSpec((k, bn), lambda i, j: (0, j + off))],
        out_specs=pl.BlockSpec((bm, bn), lambda i, j: (i, j)),
        out_shape=jax.ShapeDtypeStruct((m, n), _F32),
        compiler_params=_params("parallel", "parallel"),
    )(x, w)


def _norm_residual_kernel(y_ref, x_ref, gpost_ref, gpre_ref, xo_ref, ho_ref):
    xn = x_ref[...] + _rms(y_ref[...]) * gpost_ref[...]
    xo_ref[...] = xn
    ho_ref[...] = (_rms(xn) * gpre_ref[...]).astype(ho_ref.dtype)


def _norm_residual_last_kernel(y_ref, x_ref, gpost_ref, xo_ref):
    xo_ref[...] = x_ref[...] + _rms(y_ref[...]) * gpost_ref[...]


def _norm_residual(y, x, gpost, gpre, h_dtype):
    m, d = x.shape
    bm = min(m, 256)
    row = pl.BlockSpec((bm, d), lambda i: (i, 0))
    vec = pl.BlockSpec((1, d), lambda i: (0, 0))
    if gpre is None:
        return pl.pallas_call(
            _norm_residual_last_kernel, grid=(m // bm,),
            in_specs=[row, row, vec], out_specs=row,
            out_shape=jax.ShapeDtypeStruct((m, d), _F32),
            compiler_params=_params("parallel"),
        )(y, x, gpost.reshape(1, d)), None
    return pl.pallas_call(
        _norm_residual_kernel, grid=(m // bm,),
        in_specs=[row, row, vec, vec], out_specs=[row, row],
        out_shape=[jax.ShapeDtypeStruct((m, d), _F32), jax.ShapeDtypeStruct((m, d), h_dtype)],
        compiler_params=_params("parallel"),
    )(y, x, gpost.reshape(1, d), gpre.reshape(1, d))


def _moba_prompt_kernel(slopes_ref, q_ref, k_ref, v_ref, g_ref, o_ref, *, n_blocks):
    blk = MOBA_BLOCK
    seq = n_blocks * blk
    slope = slopes_ref[pl.program_id(1)]
    scale = ATTN_HEAD_DIM ** -0.5
    k = k_ref[0]
    qb = q_ref[0].astype(_BF16)
    kb = k.astype(_BF16)
    vb = v_ref[0].astype(_BF16)

    nbp = -(-n_blocks // V7X_SUBLANES) * V7X_SUBLANES
    means = [jnp.sum(k[n * blk:(n + 1) * blk], axis=0, keepdims=True) * (1.0 / blk) for n in range(n_blocks)]
    if nbp > n_blocks:
        means.append(jnp.zeros((nbp - n_blocks, ATTN_HEAD_DIM), _F32))
    kmean = jnp.concatenate(means, axis=0)
    gate = lax.dot_general(kmean.astype(_BF16), qb, _NT, preferred_element_type=_F32)
    blk_idx = lax.broadcasted_iota(jnp.int32, (nbp, seq), 0)
    q_blk = jnp.right_shift(lax.broadcasted_iota(jnp.int32, (1, seq), 1), MOBA_BLOCK_SHIFT)
    gate = jnp.where(blk_idx < q_blk, gate, -jnp.inf)
    sel_rows = []
    for n in range(n_blocks):
        g_n = gate[n:n + 1, :]
        beats = (gate > g_n) | ((gate == g_n) & (blk_idx < n))
        rank = jnp.sum(beats.astype(_F32), axis=0, keepdims=True)
        sel_rows.append(((rank < MOBA_TOPK) & (q_blk > n)).astype(_F32))
    sel_rows.append(jnp.zeros((V7X_LANES - n_blocks, seq), _F32))
    sel_t = jnp.concatenate(sel_rows, axis=0)

    row = lax.broadcasted_iota(jnp.int32, (blk, blk), 0)
    col = lax.broadcasted_iota(jnp.int32, (blk, blk), 1)
    rel = row - col
    for i in range(n_blocks):
        rows = slice(i * blk, (i + 1) * blk)
        q_i = qb[rows]
        sel_i = sel_t[:, rows].T
        tiles = []
        for n in range(i + 1):
            s = lax.dot_general(q_i, kb[n * blk:(n + 1) * blk], _NT, preferred_element_type=_F32) * scale
            s = s - slope * (rel + (i - n) * blk).astype(_F32)
            keep = (jnp.broadcast_to(sel_i[:, n:n + 1], (blk, blk)) > 0.5) if n < i else (rel >= 0)
            tiles.append(jnp.where(keep, s, -jnp.inf))
        m = functools.reduce(jnp.maximum, [jnp.max(t, axis=1, keepdims=True) for t in tiles])
        probs = [jnp.exp(t - m) for t in tiles]
        inv = 1.0 / sum(jnp.sum(p, axis=1, keepdims=True) for p in probs)
        o = sum(jnp.dot((p * inv).astype(_BF16), vb[n * blk:(n + 1) * blk], preferred_element_type=_F32)
                for n, p in enumerate(probs))
        o_ref[0, rows, :] = (o * _silu(g_ref[0, rows, :])).astype(o_ref.dtype)


def _moba_prompt(q, k, v, g, slopes):
    b, s, width = q.shape
    heads = width // ATTN_HEAD_DIM
    assert s % MOBA_BLOCK == 0 and s // MOBA_BLOCK <= V7X_LANES
    spec = pl.BlockSpec((1, s, ATTN_HEAD_DIM), lambda bi, hi: (bi, 0, hi))
    return pl.pallas_call(
        functools.partial(_moba_prompt_kernel, n_blocks=s // MOBA_BLOCK),
        grid=(b, heads),
        in_specs=[pl.BlockSpec(memory_space=pltpu.SMEM), spec, spec, spec, spec],
        out_specs=spec,
        out_shape=jax.ShapeDtypeStruct((b, s, width), _BF16),
        compiler_params=_params("parallel", "parallel"),
    )(slopes, q, k, v, g)


def _sample_select_kernel(pt_ref, q_ref, p0_ref, p1_ref, sel_ref, gate_ref):
    n = pl.program_id(1)
    n_blocks = pl.num_programs(1)
    lane = lax.broadcasted_iota(jnp.int32, gate_ref.shape, 1)

    @pl.when(n == 0)
    def _():
        gate_ref[...] = jnp.full(gate_ref.shape, -jnp.inf, _F32)

    kmean = (jnp.sum(p0_ref[0, 0], axis=0) + jnp.sum(p1_ref[0, 0], axis=0)) * (1.0 / MOBA_BLOCK)
    prod = q_ref[0].astype(_BF16).astype(_F32) * kmean.astype(_BF16).astype(_F32)
    gate_ref[...] = jnp.where(lane == n, jnp.sum(prod, axis=1, keepdims=True), gate_ref[...])

    @pl.when(n == n_blocks - 1)
    def _():
        gate = gate_ref[...]
        lane_f = lane.astype(_F32)
        sel = jnp.zeros(gate.shape, _F32)
        for kk in range(MOBA_TOPK):
            best = jnp.max(gate, axis=1, keepdims=True)
            idx = jnp.min(jnp.where(gate == best, lane_f, float(V7X_LANES)), axis=1, keepdims=True)
            sel = jnp.where(lane == kk, idx, sel)
            gate = jnp.where(lane_f == idx, -jnp.inf, gate)
        sel_ref[0] = sel.astype(jnp.int32)


def _sample_select(q, cache, layer, page_table):
    db, heads, dh = q.shape
    n_blocks = page_table.shape[1] // PAGES_PER_BLOCK
    assert MOBA_TOPK <= n_blocks <= V7X_LANES
    page = (1, 1, PAGE_SIZE, heads, dh)
    grid_spec = pltpu.PrefetchScalarGridSpec(
        num_scalar_prefetch=1,
        grid=(db, n_blocks),
        in_specs=[pl.BlockSpec((1, heads, dh), lambda b, n, pt: (b, 0, 0)),
                  pl.BlockSpec(page, lambda b, n, pt: (layer, pt[b, PAGES_PER_BLOCK * n], 0, 0, 0)),
                  pl.BlockSpec(page, lambda b, n, pt: (layer, pt[b, PAGES_PER_BLOCK * n + 1], 0, 0, 0))],
        out_specs=pl.BlockSpec((1, heads, V7X_LANES), lambda b, n, pt: (b, 0, 0)),
        scratch_shapes=[pltpu.VMEM((heads, V7X_LANES), _F32)],
    )
    sel = pl.pallas_call(
        _sample_select_kernel, grid_spec=grid_spec,
        out_shape=jax.ShapeDtypeStruct((db, heads, V7X_LANES), jnp.int32),
        compiler_params=_params("parallel", "arbitrary"),
    )(page_table, q, cache, cache)
    return sel[:, :, :MOBA_TOPK].reshape(-1)


def _sample_attn_kernel(sel_ref, pt_ref, slopes_ref, q_ref, kn_ref, vn_ref, g_ref, ck_hbm, cv_hbm, o_ref,
                        kbuf, vbuf, sem, *, heads, past_len, layer):
    hg = pl.program_id(0)
    b = pl.program_id(1)
    group = kbuf.shape[0]
    n_slots = MOBA_TOPK * PAGES_PER_BLOCK
    scale = ATTN_HEAD_DIM ** -0.5

    @pl.when(b == 0)
    def _():
        o_ref[...] = jnp.zeros(o_ref.shape, _F32)

    def page_copies(hh):
        h = hg * group + hh
        out = []
        for kk in range(MOBA_TOPK):
            blk = sel_ref[(b * heads + h) * MOBA_TOPK + kk]
            for j in range(PAGES_PER_BLOCK):
                page = pt_ref[b, blk * PAGES_PER_BLOCK + j]
                slot = kk * PAGES_PER_BLOCK + j
                rows = pl.ds(slot * PAGE_SIZE, PAGE_SIZE)
                out.append(pltpu.make_async_copy(ck_hbm.at[layer, page, :, h, :], kbuf.at[hh, rows, :], sem.at[0, hh, slot]))
                out.append(pltpu.make_async_copy(cv_hbm.at[layer, page, :, h, :], vbuf.at[hh, rows, :], sem.at[1, hh, slot]))
        return out

    for hh in range(group):
        for c in page_copies(hh):
            c.start()

    lane = lax.broadcasted_iota(jnp.int32, (1, n_slots * PAGE_SIZE), 1)
    which = jnp.right_shift(lane, MOBA_BLOCK_SHIFT)
    within = jnp.bitwise_and(lane, MOBA_BLOCK - 1)
    for hh in range(group):
        for c in page_copies(hh):
            c.wait()
        h = hg * group + hh
        cols = slice(hh * ATTN_HEAD_DIM, (hh + 1) * ATTN_HEAD_DIM)
        q = _take_row(q_ref[:, cols], b)
        q8 = jnp.broadcast_to(q, (V7X_SUBLANES, ATTN_HEAD_DIM)).astype(_BF16)
        s = lax.dot_general(q8, kbuf[hh].astype(_BF16), _NT, preferred_element_type=_F32)[0:1] * scale
        blk_of = jnp.zeros_like(lane)
        for kk in range(MOBA_TOPK):
            blk_of = jnp.where(which == kk, sel_ref[(b * heads + h) * MOBA_TOPK + kk], blk_of)
        dist = past_len - (blk_of * MOBA_BLOCK + within)
        s = s - slopes_ref[h] * dist.astype(_F32)
        qf = q.astype(_BF16).astype(_F32)
        knf = _take_row(kn_ref[:, cols], b).astype(_BF16).astype(_F32)
        s_own = jnp.sum(qf * knf, axis=1, keepdims=True) * scale
        m = jnp.maximum(jnp.max(s, axis=1, keepdims=True), s_own)
        p = jnp.exp(s - m)
        p_own = jnp.exp(s_own - m)
        inv = 1.0 / (jnp.sum(p, axis=1, keepdims=True) + p_own)
        p8 = jnp.broadcast_to(p * inv, (V7X_SUBLANES, n_slots * PAGE_SIZE)).astype(_BF16)
        o = jnp.dot(p8, vbuf[hh].astype(_BF16), preferred_element_type=_F32)[0:1]
        vnf = _take_row(vn_ref[:, cols], b).astype(_BF16).astype(_F32)
        o = o + (p_own * inv).astype(_BF16).astype(_F32) * vnf
        _put_row(o_ref, cols, b, o * _silu(_take_row(g_ref[:, cols], b)))


def _sample_attn(q, k_new, v_new, g, cache_k, cache_v, layer, sel, page_table, slopes):
    db, width = q.shape
    heads = width // ATTN_HEAD_DIM
    group = min(SAMPLE_HEAD_GROUP, heads)
    assert heads % group == 0
    n_slots = MOBA_TOPK * PAGES_PER_BLOCK
    past_len = page_table.shape[1] * PAGE_SIZE
    vec = pl.BlockSpec((db, group * ATTN_HEAD_DIM), lambda hg, b, *_: (0, hg))
    grid_spec = pltpu.PrefetchScalarGridSpec(
        num_scalar_prefetch=2,
        grid=(heads // group, db),
        in_specs=[pl.BlockSpec(memory_space=pltpu.SMEM), vec, vec, vec, vec,
                  pl.BlockSpec(memory_space=pl.ANY), pl.BlockSpec(memory_space=pl.ANY)],
        out_specs=vec,
        scratch_shapes=[pltpu.VMEM((group, n_slots * PAGE_SIZE, ATTN_HEAD_DIM), _F32),
                        pltpu.VMEM((group, n_slots * PAGE_SIZE, ATTN_HEAD_DIM), _F32),
                        pltpu.SemaphoreType.DMA((2, group, n_slots))],
    )
    return pl.pallas_call(
        functools.partial(_sample_attn_kernel, heads=heads, past_len=past_len, layer=layer),
        grid_spec=grid_spec,
        out_shape=jax.ShapeDtypeStruct((db, width), _F32),
        compiler_params=_params("arbitrary", "arbitrary"),
    )(sel, page_table, slopes, q, k_new, v_new, g, cache_k, cache_v)


def _retention_prompt_kernel(lg_ref, q_ref, k_ref, v_ref, g_ref, gn_ref, o_ref, st_ref, *, n_chunks):
    c = RET_CHUNK
    lgam = lg_ref[pl.program_id(1)]
    ii = lax.broadcasted_iota(jnp.int32, (c, c), 0)
    jj = lax.broadcasted_iota(jnp.int32, (c, c), 1)
    causal = ii >= jj
    diff = jnp.where(causal, ii - jj, 0).astype(_F32)
    decay = jnp.where(causal, jnp.exp(diff * lgam), 0.0)
    idx = lax.broadcasted_iota(jnp.int32, (c, 1), 0).astype(_F32)
    q_decay = jnp.exp((idx + 1.0) * lgam)
    k_decay = jnp.exp((c - 1.0 - idx) * lgam)
    chunk_decay = jnp.exp(jnp.full((1, 1), float(c), _F32) * lgam)
    st_ref[0, 0] = jnp.zeros(st_ref.shape[2:], _F32)

    def chunk(ci, carry):
        rows = pl.ds(pl.multiple_of(ci * c, c), c)
        qc = q_ref[0, rows, :].astype(_BF16)
        kc = k_ref[0, rows, :] * (RET_KEY_DIM ** -0.5)
        vc = v_ref[0, rows, :].astype(_BF16)
        state = st_ref[0, 0]
        scores = lax.dot_general(qc, kc.astype(_BF16), _NT, preferred_element_type=_F32) * decay
        o = jnp.dot(scores.astype(_BF16), vc, preferred_element_type=_F32)
        o = o + jnp.dot(qc, state.astype(_BF16), preferred_element_type=_F32) * q_decay
        update = lax.dot_general((kc * k_decay).astype(_BF16), vc, _TN, preferred_element_type=_F32)
        st_ref[0, 0] = chunk_decay * state + update
        gate = g_ref[0, rows, :]
        o_ref[0, rows, :] = (_rms(o) * gn_ref[...] * _silu(gate)).astype(o_ref.dtype)
        return carry

    lax.fori_loop(0, n_chunks, chunk, 0)


def _retention_prompt(q, k, v, g, g_norm, log_gamma):
    b, s, qk_width = q.shape
    heads = qk_width // RET_KEY_DIM
    assert s % RET_CHUNK == 0
    qk_spec = pl.BlockSpec((1, s, RET_KEY_DIM), lambda bi, hi: (bi, 0, hi))
    v_spec = pl.BlockSpec((1, s, RET_VALUE_DIM), lambda bi, hi: (bi, 0, hi))
    return pl.pallas_call(
        functools.partial(_retention_prompt_kernel, n_chunks=s // RET_CHUNK),
        grid=(b, heads),
        in_specs=[pl.BlockSpec(memory_space=pltpu.SMEM), qk_spec, qk_spec, v_spec, v_spec,
                  pl.BlockSpec((1, RET_VALUE_DIM), lambda bi, hi: (0, hi))],
        out_specs=[v_spec, pl.BlockSpec((1, 1, RET_KEY_DIM, RET_VALUE_DIM), lambda bi, hi: (bi, hi, 0, 0))],
        out_shape=[jax.ShapeDtypeStruct((b, s, heads * RET_VALUE_DIM), _BF16),
                   jax.ShapeDtypeStruct((b, heads, RET_KEY_DIM, RET_VALUE_DIM), _F32)],
        compiler_params=_params("parallel", "parallel"),
    )(log_gamma, q, k, v, g, g_norm.reshape(1, -1))


def _retention_sample_kernel(lg_ref, q_ref, k_ref, v_ref, g_ref, gn_ref, st_ref, o_ref, so_ref):
    lgam = lg_ref[pl.program_id(0)]
    gamma = jnp.exp(jnp.full((1, 1), 1.0, _F32) * lgam)
    b = pl.program_id(1)

    @pl.when(b == 0)
    def _():
        o_ref[...] = jnp.zeros(o_ref.shape, _F32)

    q = _take_row(q_ref[...], b)
    k = _take_row(k_ref[...], b) * (RET_KEY_DIM ** -0.5)
    v = _take_row(v_ref[...], b)
    state = st_ref[0, 0]
    qf = q.astype(_BF16).astype(_F32)
    kf = k.astype(_BF16).astype(_F32)
    vf = v.astype(_BF16).astype(_F32)
    score = jnp.sum(qf * kf, axis=1, keepdims=True)
    o = score.astype(_BF16).astype(_F32) * vf
    q8 = jnp.broadcast_to(q, (V7X_SUBLANES, RET_KEY_DIM)).astype(_BF16)
    o = o + jnp.dot(q8, state.astype(_BF16), preferred_element_type=_F32)[0:1] * gamma
    first = lax.broadcasted_iota(jnp.int32, (V7X_SUBLANES, 1), 0) == 0
    k8 = jnp.where(first, jnp.broadcast_to(k, (V7X_SUBLANES, RET_KEY_DIM)), 0.0).astype(_BF16)
    v8 = jnp.broadcast_to(v, (V7X_SUBLANES, RET_VALUE_DIM)).astype(_BF16)
    so_ref[0, 0] = gamma * state + lax.dot_general(k8, v8, _TN, preferred_element_type=_F32)
    _put_row(o_ref, slice(None), b, _rms(o) * gn_ref[...] * _silu(_take_row(g_ref[...], b)))


def _retention_sample(q, k, v, g, g_norm, state, log_gamma):
    db, qk_width = q.shape
    heads = qk_width // RET_KEY_DIM
    qk_spec = pl.BlockSpec((db, RET_KEY_DIM), lambda h, b: (0, h))
    v_spec = pl.BlockSpec((db, RET_VALUE_DIM), lambda h, b: (0, h))
    st_spec = pl.BlockSpec((1, 1, RET_KEY_DIM, RET_VALUE_DIM), lambda h, b: (b, h, 0, 0))
    return pl.pallas_call(
        _retention_sample_kernel,
        grid=(heads, db),
        in_specs=[pl.BlockSpec(memory_space=pltpu.SMEM), qk_spec, qk_spec, v_spec, v_spec,
                  pl.BlockSpec((1, RET_VALUE_DIM), lambda h, b: (0, h)), st_spec],
        out_specs=[v_spec, st_spec],
        out_shape=[jax.ShapeDtypeStruct((db, heads * RET_VALUE_DIM), _F32),
                   jax.ShapeDtypeStruct(state.shape, _F32)],
        compiler_params=_params("arbitrary", "arbitrary"),
    )(log_gamma, q, k, v, g, g_norm.reshape(1, -1), state)


def kernel(x_prompt, x_sample, cache_k, cache_v, state_ret, page_table, norm_pre, norm_post,
           w_in_attn, w_out_attn, w_in_ret, w_out_ret, ret_norm):
    bp, sp, d = x_prompt.shape
    bs, ts, _ = x_sample.shape
    depth = norm_pre.shape[0]
    attn_width = w_out_attn.shape[1]
    attn_heads = attn_width // ATTN_HEAD_DIM
    ret_v_width = w_out_ret.shape[1]
    ret_heads = ret_v_width // RET_VALUE_DIM
    ret_qk_width = ret_heads * RET_KEY_DIM
    assert ts == 1 and page_table.shape[1] % PAGES_PER_BLOCK == 0
    assert cache_k.shape[2:] == (PAGE_SIZE, attn_heads, ATTN_HEAD_DIM)
    assert w_in_ret.shape[2] == 2 * ret_qk_width + 2 * ret_v_width

    slopes = jnp.exp2(-8.0 * jnp.arange(1, attn_heads + 1, dtype=_F32) / attn_heads)
    log_gamma = jnp.log1p(-jnp.exp2(-5.0 - jnp.arange(ret_heads, dtype=_F32)))

    xp = x_prompt.reshape(bp * sp, d)
    xs = x_sample.reshape(bs * ts, d)
    hp = _rmsnorm(xp, norm_pre[0], _BF16)
    hs = _rmsnorm(xs, norm_pre[0], _F32)
    k_prompt, v_prompt, k_sample, v_sample, st_prompt, st_sample = [], [], [], [], [], []
    for i in range(depth):
        j = i // 2
        gpre = norm_pre[i + 1] if i + 1 < depth else None
        if i % 2 == 0:
            w_in = w_in_attn[j].astype(_BF16)
            w_out = w_out_attn[j].astype(_BF16)
            q, k, v, g = (_matmul(hp, w_in, c * attn_width, attn_width).reshape(bp, sp, attn_width) for c in range(4))
            ap = _moba_prompt(q, k, v, g, slopes).reshape(bp * sp, attn_width)
            k_prompt.append(k.reshape(bp, sp, attn_heads, ATTN_HEAD_DIM))
            v_prompt.append(v.reshape(bp, sp, attn_heads, ATTN_HEAD_DIM))

            q, k, v, g = (_matmul(hs, w_in, c * attn_width, attn_width) for c in range(4))
            sel = _sample_select(q.reshape(bs, attn_heads, ATTN_HEAD_DIM), cache_k, j, page_table)
            a_s = _sample_attn(q, k, v, g, cache_k, cache_v, j, sel, page_table, slopes)
            k_sample.append(k.reshape(bs, ts, attn_heads, ATTN_HEAD_DIM))
            v_sample.append(v.reshape(bs, ts, attn_heads, ATTN_HEAD_DIM))
        else:
            w_in = w_in_ret[j].astype(_BF16)
            w_out = w_out_ret[j].astype(_BF16)
            cols = (0, ret_qk_width, 2 * ret_qk_width, 2 * ret_qk_width + ret_v_width)
            widths = (ret_qk_width, ret_qk_width, ret_v_width, ret_v_width)
            q, k, v, g = (_matmul(hp, w_in, c0, n).reshape(bp, sp, n) for c0, n in zip(cols, widths))
            ap, stp = _retention_prompt(q, k, v, g, ret_norm[j], log_gamma)
            ap = ap.reshape(bp * sp, ret_v_width)
            st_prompt.append(stp)

            q, k, v, g = (_matmul(hs, w_in, c0, n) for c0, n in zip(cols, widths))
            a_s, sts = _retention_sample(q, k, v, g, ret_norm[j], state_ret[j], log_gamma)
            st_sample.append(sts)
        xp, hp = _norm_residual(_matmul(ap, w_out, 0, d), xp, norm_post[i], gpre, _BF16)
        xs, hs = _norm_residual(_matmul(a_s, w_out, 0, d), xs, norm_post[i], gpre, _F32)
    return (xp.reshape(bp, sp, d), xs.reshape(bs, ts, d),
            jnp.stack(k_prompt), jnp.stack(v_prompt), jnp.stack(st_prompt),
            jnp.stack(k_sample), jnp.stack(v_sample), jnp.stack(st_sample))
```
